```python
import math
import numpy as np
import jax
import jax.numpy as jnp
from jax import lax

D_MODEL = 1024
BATCH = 8
SEQ = 2048
DEPTH = 4
DEC_BATCH = 128
DEC_SEQ = 4
PAST_LEN = 2048
PAGE_SIZE = 128

N_HEADS_A = 16
N_KV_A = 4
HEAD_DIM_A = 64
REP_A = N_HEADS_A // N_KV_A
CMP_BLOCK = 32
CMP_STRIDE = 16
CMP_HIDDEN = 256
SEL_BLOCK = 64
N_SELECT = 8
WINDOW = 512
Q_BLOCK_A = 64
HEAD_DIM_B = 64
N_HEADS_B = D_MODEL // (2 * HEAD_DIM_B)
N_KV_B = 4
REP_B = N_HEADS_B // N_KV_B
Q_BLOCK_B = 128
D_FF = 4 * D_MODEL
PLE_DIM = 256
ROPE_THETA = 10000.0
ALPHA = (2 * DEPTH) ** 0.25
BETA = (8 * DEPTH) ** -0.25
LN_EPS = 1e-5
NEG_INF = -1e30
FORCE_SCORE = 1e4
N_LAYERS_A = (DEPTH + 1) // 2
N_LAYERS_B = DEPTH // 2
A_Q = N_HEADS_A * HEAD_DIM_A
A_KV = N_KV_A * HEAD_DIM_A
A_IN = A_Q + 6 * A_KV + 3 * N_HEADS_A
B_Q = N_HEADS_B * 2 * HEAD_DIM_B
B_KV = N_KV_B * 2 * HEAD_DIM_B
B_IN = B_Q + 2 * B_KV

kernel_name = 'nsa_diffattn_hybrid_step'


def layer_norm(x, g, b):
    xf = x.astype(jnp.float32)
    mu = xf.mean(-1, keepdims=True)
    var = jnp.square(xf - mu).mean(-1, keepdims=True)
    return ((xf - mu) * lax.rsqrt(var + LN_EPS) * g.astype(jnp.float32) + b.astype(jnp.float32)).astype(x.dtype)


def rope(x, pos):
    half = x.shape[-1] // 2
    inv = ROPE_THETA ** (-jnp.arange(half, dtype=jnp.float32) / half)
    ang = pos.astype(jnp.float32)[:, None] * inv[None, :]
    shape = (1, pos.shape[0]) + (1,) * (x.ndim - 3) + (half,)
    cos = jnp.cos(ang).reshape(shape).astype(x.dtype)
    sin = jnp.sin(ang).reshape(shape).astype(x.dtype)
    x1, x2 = x[..., :half], x[..., half:]
    return jnp.concatenate([x1 * cos - x2 * sin, x2 * cos + x1 * sin], axis=-1)


def masked_softmax(s, mask):
    p = jax.nn.softmax(jnp.where(mask, s.astype(jnp.float32), NEG_INF), axis=-1)
    return p * mask


def gather_pages(pool, page_table):
    g = pool[page_table]
    return g.reshape((g.shape[0], g.shape[1] * g.shape[2]) + g.shape[3:])


def compress_blocks(k, w1, b1, pe, w2):
    B, L, G, d = k.shape
    r = CMP_BLOCK // CMP_STRIDE
    nch = L // CMP_STRIDE
    nc = nch - r + 1
    chunks = k[:, :nch * CMP_STRIDE].reshape(B, nch, CMP_STRIDE, G, d)
    part = jnp.einsum('bcsgd,msdh->bmcgh', chunks, w1)
    pre = jnp.einsum('msd,msdh->h', pe.reshape(r, CMP_STRIDE, d), w1) + b1
    for m in range(r):
        pre = pre + part[:, m, m:m + nc]
    return jnp.einsum('bngh,hd->bngd', jax.nn.gelu(pre), w2)


def nsa_attend(q, gates, kc, vc, ks_blk, vs_blk, kw_pad, vw_pad, pos0, win_pos0):
    B, T = q.shape[0], q.shape[1]
    qb_len = min(Q_BLOCK_A, T)
    nc = kc.shape[1]
    ns = ks_blk.shape[2]
    k_sel = min(N_SELECT, ns)
    c_start = np.arange(nc) * CMP_STRIDE
    c_end = jnp.asarray(c_start + CMP_BLOCK - 1)
    j_np = np.arange(ns)
    ovl = jnp.asarray(((c_start[:, None] < (j_np[None, :] + 1) * SEL_BLOCK)
                       & (c_start[:, None] + CMP_BLOCK > j_np[None, :] * SEL_BLOCK)).astype(np.float32))
    blk_id = jnp.arange(ns)
    bi = jnp.arange(B)[:, None, None, None]
    gi = jnp.arange(N_KV_A)[None, :, None, None]
    span = qb_len + WINDOW

    def block(i):
        q0 = i * qb_len
        qb = lax.dynamic_slice_in_dim(q, q0, qb_len, 1).reshape(B, qb_len, N_KV_A, REP_A, HEAD_DIM_A)
        gb = lax.dynamic_slice_in_dim(gates, q0, qb_len, 1)
        q_abs = pos0 + q0 + jnp.arange(qb_len)
        s_c = jnp.einsum('bqgrd,bngd->bqgrn', qb, kc)
        mask_c = (c_end[None, :] <= q_abs[:, None])[None, :, None, None, :]
        p_c = masked_softmax(s_c, mask_c)
        o_c = jnp.einsum('bqgrn,bngd->bqgrd', p_c.astype(vc.dtype), vc)
        imp = jnp.einsum('bqgn,ns->bqgs', p_c.sum(axis=3), ovl)
        cur = q_abs // SEL_BLOCK
        forced = (blk_id[None, :] == 0) | (blk_id[None, :] == cur[:, None]) | (blk_id[None, :] == cur[:, None] - 1)
        valid = blk_id[None, :] * SEL_BLOCK <= q_abs[:, None]
        score = jnp.where(valid[None, :, None, :], imp + FORCE_SCORE * forced[None, :, None, :],
                          -FORCE_SCORE * FORCE_SCORE)
        _, idx = lax.top_k(score, k_sel)
        idx = idx.transpose(0, 2, 1, 3)
        kg = ks_blk[bi, gi, idx]
        vg = vs_blk[bi, gi, idx]
        s_s = jnp.einsum('bqgrd,bgqkpd->bqgrkp', qb, kg).reshape(B, qb_len, N_KV_A, REP_A, k_sel * SEL_BLOCK)
        key_pos = idx[..., None] * SEL_BLOCK + jnp.arange(SEL_BLOCK)
        mask_s = (key_pos <= q_abs[None, None, :, None, None]).transpose(0, 2, 1, 3, 4).reshape(
            B, qb_len, N_KV_A, 1, k_sel * SEL_BLOCK)
        p_s = masked_softmax(s_s, mask_s).reshape(B, qb_len, N_KV_A, REP_A, k_sel, SEL_BLOCK)
        o_s = jnp.einsum('bqgrkp,bgqkpd->bqgrd', p_s.astype(vg.dtype), vg)
        start = pos0 + q0 - win_pos0
        kw = lax.dynamic_slice_in_dim(kw_pad, start, span, 1)
        vw = lax.dynamic_slice_in_dim(vw_pad, start, span, 1)
        k_abs = pos0 + q0 - WINDOW + jnp.arange(span)
        mask_w = ((k_abs[None, :] <= q_abs[:, None]) & (k_abs[None, :] >= q_abs[:, None] - WINDOW)
                  & (k_abs[None, :] >= win_pos0))[None, :, None, None, :]
        s_w = jnp.einsum('bqgrd,bsgd->bqgrs', qb, kw)
        p_w = masked_softmax(s_w, mask_w)
        o_w = jnp.einsum('bqgrs,bsgd->bqgrd', p_w.astype(vw.dtype), vw)
        g = gb.reshape(B, qb_len, 3, N_KV_A, REP_A, 1)
        o = g[:, :, 0] * o_c + g[:, :, 1] * o_s + g[:, :, 2] * o_w
        return o.reshape(B, qb_len, N_HEADS_A * HEAD_DIM_A)

    out = lax.map(block, jnp.arange(T // qb_len))
    return out.transpose(1, 0, 2, 3).reshape(B, T, N_HEADS_A * HEAD_DIM_A)


def nsa_mixer(x, pos0, past_kc, past_vc, past_ks, past_vs, buf_kw, buf_vw, win_keep,
              w_in, ck_w1, ck_b1, ck_pe, ck_w2, cv_w1, cv_b1, cv_pe, cv_w2, w_out):
    B, T, _ = x.shape
    pos = pos0 + jnp.arange(T)
    proj = x @ w_in
    q = rope(proj[..., :A_Q].reshape(B, T, N_HEADS_A, HEAD_DIM_A), pos) * HEAD_DIM_A ** -0.5
    kv = proj[..., A_Q:A_Q + 6 * A_KV].reshape(B, T, 6, N_KV_A, HEAD_DIM_A)
    kc_new, vc_new = rope(kv[:, :, 0], pos), kv[:, :, 1]
    ks_new, vs_new = rope(kv[:, :, 2], pos), kv[:, :, 3]
    kw_new, vw_new = rope(kv[:, :, 4], pos), kv[:, :, 5]
    gates = jax.nn.sigmoid(proj[..., A_Q + 6 * A_KV:].reshape(B, T, 3, N_HEADS_A))
    k_cmp = jnp.concatenate([past_kc, kc_new], 1)
    v_cmp = jnp.concatenate([past_vc, vc_new], 1)
    k_sel = jnp.concatenate([past_ks, ks_new], 1)
    v_sel = jnp.concatenate([past_vs, vs_new], 1)
    L = k_cmp.shape[1]
    kc = compress_blocks(k_cmp, ck_w1, ck_b1, ck_pe, ck_w2)
    vc = compress_blocks(v_cmp, cv_w1, cv_b1, cv_pe, cv_w2)
    ns = -(-L // SEL_BLOCK)

    def to_blocks(t):
        t = jnp.pad(t, ((0, 0), (0, ns * SEL_BLOCK - L), (0, 0), (0, 0)))
        return t.reshape(B, ns, SEL_BLOCK, N_KV_A, HEAD_DIM_A).transpose(0, 3, 1, 2, 4)

    def pad_win(t):
        return jnp.pad(t, ((0, 0), (WINDOW, 0), (0, 0), (0, 0)))

    k_win = jnp.concatenate([buf_kw, kw_new], 1)
    v_win = jnp.concatenate([buf_vw, vw_new], 1)
    win_pos0 = pos0 - buf_kw.shape[1]
    o = nsa_attend(q, gates, kc, vc, to_blocks(k_sel), to_blocks(v_sel), pad_win(k_win), pad_win(v_win),
                   pos0, win_pos0)
    keep_from = k_win.shape[1] - win_keep
    return (o @ w_out, kc_new, vc_new, ks_new, vs_new, k_win[:, keep_from:], v_win[:, keep_from:])


def diff_mixer(x, pos0, past_k, past_v, lam_init, w_in, lam_vecs, subln_g, w_out):
    B, T, _ = x.shape
    pos = pos0 + jnp.arange(T)
    proj = x @ w_in
    q = rope(proj[..., :B_Q].reshape(B, T, N_HEADS_B, 2, HEAD_DIM_B), pos) * HEAD_DIM_B ** -0.5
    k = rope(proj[..., B_Q:B_Q + B_KV].reshape(B, T, N_KV_B, 2, HEAD_DIM_B), pos)
    k_new = k.reshape(B, T, N_KV_B, 2 * HEAD_DIM_B)
    v_new = proj[..., B_Q + B_KV:].reshape(B, T, N_KV_B, 2 * HEAD_DIM_B)
    keys = jnp.concatenate([past_k, k_new], 1)
    vals = jnp.concatenate([past_v, v_new], 1)
    L = keys.shape[1]
    keys = keys.reshape(B, L, N_KV_B, 2, HEAD_DIM_B)
    lv = lam_vecs.astype(jnp.float32)
    lam = jnp.exp(jnp.sum(lv[0] * lv[1])) - jnp.exp(jnp.sum(lv[2] * lv[3])) + lam_init
    qb_len = min(Q_BLOCK_B, T)
    k_pos = jnp.arange(L)

    def block(i):
        q0 = i * qb_len
        qb = lax.dynamic_slice_in_dim(q, q0, qb_len, 1).reshape(B, qb_len, N_KV_B, REP_B, 2, HEAD_DIM_B)
        q_abs = pos0 + q0 + jnp.arange(qb_len)
        s = jnp.einsum('bqgrcd,bkgcd->bqgrck', qb, keys)
        mask = (k_pos[None, :] <= q_abs[:, None])[None, :, None, None, None, :]
        p = masked_softmax(s, mask)
        a = p[..., 0, :] - lam * p[..., 1, :]
        return jnp.einsum('bqgrk,bkge->bqgre', a.astype(vals.dtype), vals)

    o = lax.map(block, jnp.arange(T // qb_len))
    o = o.transpose(1, 0, 2, 3, 4, 5).reshape(B, T, N_HEADS_B, 2 * HEAD_DIM_B).astype(jnp.float32)
    o = o * lax.rsqrt(jnp.mean(jnp.square(o), -1, keepdims=True) + LN_EPS) * subln_g.astype(jnp.float32) * (1.0 - lam_init)
    out = o.astype(x.dtype).reshape(B, T, N_HEADS_B * 2 * HEAD_DIM_B) @ w_out
    return out, k_new, v_new


def ffn_and_ple(x, mix, p_emb, ln1_g, ln1_b, ln2_g, ln2_b, mlp_w1, mlp_w2, ple_w, ple_gate_w):
    h = layer_norm(ALPHA * x + mix, ln1_g, ln1_b)
    f = jnp.square(jax.nn.relu(h @ mlp_w1)) @ mlp_w2
    h = layer_norm(ALPHA * h + f, ln2_g, ln2_b)
    return h + jax.nn.sigmoid(h @ ple_gate_w) * (p_emb @ ple_w)


def setup_inputs(seed: int = 0) -> dict:
    key = jax.random.key(seed)
    keys = iter(jax.random.split(key, 48))

    def nrm(shape, scale):
        return jax.random.normal(next(keys), shape, jnp.float32) * scale

    n_pages = PAST_LEN // PAGE_SIZE
    n_pool = (5 * DEC_BATCH * n_pages + 3) // 4
    win_buf = min(WINDOW, PAST_LEN)
    na, nb = N_LAYERS_A, N_LAYERS_B
    r = CMP_BLOCK // CMP_STRIDE
    a_col = jnp.asarray(np.concatenate([np.ones(A_Q), np.tile(np.repeat([1.0, BETA], A_KV), 3),
                                        np.ones(3 * N_HEADS_A)]).astype(np.float32))
    b_col = jnp.asarray(np.concatenate([np.ones(B_Q + B_KV), np.full(B_KV, BETA)]).astype(np.float32))
    pa_shape = (na, n_pool, PAGE_SIZE, N_KV_A, HEAD_DIM_A)
    pb_shape = (nb, n_pool, PAGE_SIZE, N_KV_B, 2 * HEAD_DIM_B)
    wa_shape = (na, DEC_BATCH, win_buf, N_KV_A, HEAD_DIM_A)
    page_table = jax.random.permutation(next(keys), n_pool)[:DEC_BATCH * n_pages].reshape(
        DEC_BATCH, n_pages).astype(jnp.int32)
    return {
        'x_prompt': nrm((BATCH, SEQ, D_MODEL), 1.0),
        'x_sample': nrm((DEC_BATCH, DEC_SEQ, D_MODEL), 1.0),
        'cache_a_k_cmp': nrm(pa_shape, 1.0),
        'cache_a_v_cmp': nrm(pa_shape, BETA),
        'cache_a_k_sel': nrm(pa_shape, 1.0),
        'cache_a_v_sel': nrm(pa_shape, BETA),
        'state_a_k_win': nrm(wa_shape, 1.0),
        'state_a_v_win': nrm(wa_shape, BETA),
        'cache_b_k': nrm(pb_shape, 1.0),
        'cache_b_v': nrm(pb_shape, BETA),
        'page_table': page_table,
        'p_prompt': nrm((DEPTH, BATCH, SEQ, PLE_DIM), 1.0),
        'p_sample': nrm((DEPTH, DEC_BATCH, DEC_SEQ, PLE_DIM), 1.0),
        'a_w_in': nrm((na, D_MODEL, A_IN), D_MODEL ** -0.5) * a_col,
        'a_cmpk_w1': nrm((na, r, CMP_STRIDE, HEAD_DIM_A, CMP_HIDDEN), (CMP_BLOCK * HEAD_DIM_A) ** -0.5),
        'a_cmpk_b1': nrm((na, CMP_HIDDEN), 0.02),
        'a_cmpk_pe': nrm((na, CMP_BLOCK, HEAD_DIM_A), 0.1),
        'a_cmpk_w2': nrm((na, CMP_HIDDEN, HEAD_DIM_A), CMP_HIDDEN ** -0.5),
        'a_cmpv_w1': nrm((na, r, CMP_STRIDE, HEAD_DIM_A, CMP_HIDDEN), (CMP_BLOCK * HEAD_DIM_A) ** -0.5),
        'a_cmpv_b1': nrm((na, CMP_HIDDEN), 0.02),
        'a_cmpv_pe': nrm((na, CMP_BLOCK, HEAD_DIM_A), 0.1),
        'a_cmpv_w2': nrm((na, CMP_HIDDEN, HEAD_DIM_A), CMP_HIDDEN ** -0.5),
        'a_w_out': nrm((na, A_Q, D_MODEL), BETA * A_Q ** -0.5),
        'b_w_in': nrm((nb, D_MODEL, B_IN), D_MODEL ** -0.5) * b_col,
        'b_lambda': nrm((nb, 4, HEAD_DIM_B), 0.1),
        'b_subln_g': 1.0 + nrm((nb, 2 * HEAD_DIM_B), 0.02),
        'b_w_out': nrm((nb, B_Q, D_MODEL), BETA * B_Q ** -0.5),
        'ln1_g': 1.0 + nrm((DEPTH, D_MODEL), 0.02),
        'ln1_b': nrm((DEPTH, D_MODEL), 0.02),
        'ln2_g': 1.0 + nrm((DEPTH, D_MODEL), 0.02),
        'ln2_b': nrm((DEPTH, D_MODEL), 0.02),
        'mlp_w1': nrm((DEPTH, D_MODEL, D_FF), D_MODEL ** -0.5),
        'mlp_w2': nrm((DEPTH, D_FF, D_MODEL), BETA * D_FF ** -0.5),
        'ple_w': nrm((DEPTH, PLE_DIM, D_MODEL), PLE_DIM ** -0.5),
        'ple_gate_w': nrm((DEPTH, D_MODEL, D_MODEL), D_MODEL ** -0.5),
    }


def reference(x_prompt, x_sample, cache_a_k_cmp, cache_a_v_cmp, cache_a_k_sel, cache_a_v_sel,
              state_a_k_win, state_a_v_win, cache_b_k, cache_b_v, page_table, p_prompt, p_sample,
              a_w_in, a_cmpk_w1, a_cmpk_b1, a_cmpk_pe, a_cmpk_w2, a_cmpv_w1, a_cmpv_b1, a_cmpv_pe, a_cmpv_w2,
              a_w_out, b_w_in, b_lambda, b_subln_g, b_w_out, ln1_g, ln1_b, ln2_g, ln2_b, mlp_w1, mlp_w2,
              ple_w, ple_gate_w):
    bp = x_prompt.shape[0]
    empty_a = jnp.zeros((bp, 0, N_KV_A, HEAD_DIM_A), x_prompt.dtype)
    empty_b = jnp.zeros((bp, 0, N_KV_B, 2 * HEAD_DIM_B), x_prompt.dtype)
    keep_p = min(WINDOW, x_prompt.shape[1])
    keep_s = state_a_k_win.shape[2]
    xp, xs = x_prompt, x_sample
    new_a_p, new_a_s, new_b_p, new_b_s = [], [], [], []
    for i in range(DEPTH):
        j = i // 2
        if i % 2 == 0:
            wa = (a_w_in[j], a_cmpk_w1[j], a_cmpk_b1[j], a_cmpk_pe[j], a_cmpk_w2[j],
                  a_cmpv_w1[j], a_cmpv_b1[j], a_cmpv_pe[j], a_cmpv_w2[j], a_w_out[j])
            res_p = nsa_mixer(xp, 0, empty_a, empty_a, empty_a, empty_a, empty_a, empty_a, keep_p, *wa)
            res_s = nsa_mixer(xs, PAST_LEN,
                              gather_pages(cache_a_k_cmp[j], page_table), gather_pages(cache_a_v_cmp[j], page_table),
                              gather_pages(cache_a_k_sel[j], page_table), gather_pages(cache_a_v_sel[j], page_table),
                              state_a_k_win[j], state_a_v_win[j], keep_s, *wa)
            mix_p, mix_s = res_p[0], res_s[0]
            new_a_p.append(res_p[1:])
            new_a_s.append(res_s[1:])
        else:
            lam_init = 0.8 - 0.6 * math.exp(-0.3 * i)
            wb = (b_w_in[j], b_lambda[j], b_subln_g[j], b_w_out[j])
            mix_p, kp, vp = diff_mixer(xp, 0, empty_b, empty_b, lam_init, *wb)
            mix_s, ks, vs = diff_mixer(xs, PAST_LEN, gather_pages(cache_b_k[j], page_table),
                                       gather_pages(cache_b_v[j], page_table), lam_init, *wb)
            new_b_p.append((kp, vp))
            new_b_s.append((ks, vs))
        lw = (ln1_g[i], ln1_b[i], ln2_g[i], ln2_b[i], mlp_w1[i], mlp_w2[i], ple_w[i], ple_gate_w[i])
        xp = ffn_and_ple(xp, mix_p, p_prompt[i], *lw)
        xs = ffn_and_ple(xs, mix_s, p_sample[i], *lw)
    p_a_kc, p_a_vc, p_a_ks, p_a_vs, p_a_kw, p_a_vw = [jnp.stack(t) for t in zip(*new_a_p)]
    s_a_kc, s_a_vc, s_a_ks, s_a_vs, s_a_kw, s_a_vw = [jnp.stack(t) for t in zip(*new_a_s)]
    p_b_k, p_b_v = [jnp.stack(t) for t in zip(*new_b_p)]
    s_b_k, s_b_v = [jnp.stack(t) for t in zip(*new_b_s)]
    return (xp, xs, p_a_kc, p_a_vc, p_a_ks, p_a_vs, p_a_kw, p_a_vw, p_b_k, p_b_v,
            s_a_kc, s_a_vc, s_a_ks, s_a_vs, s_a_kw, s_a_vw, s_b_k, s_b_v)
```

```python
import functools
import math

import numpy as np
import jax
import jax.numpy as jnp
from jax import lax
from jax.experimental import pallas as pl
from jax.experimental.pallas import tpu as pltpu

F32 = jnp.float32
BF16 = jnp.bfloat16

LANES = 128
SUBLANES = 8
VMEM_LIMIT_BYTES = 56 * 1024 * 1024

PAGE_SIZE = 128
N_HEADS_A = 16
N_KV_A = 4
HEAD_DIM_A = 64
REP_A = N_HEADS_A // N_KV_A
CMP_BLOCK = 32
CMP_STRIDE = 16
SEL_BLOCK = 64
N_SELECT = 8
WINDOW = 512
HEAD_DIM_B = 64
N_KV_B = 4
ROPE_THETA = 10000.0
LN_EPS = 1e-5
NEG_INF = -1e30
FORCE_SCORE = 1e4

A_KV = N_KV_A * HEAD_DIM_A
B_KV = N_KV_B * 2 * HEAD_DIM_B

TOKEN_TILE = 512
Q_TILE = 128
K_CHUNK = 256
DEC_ROWS = 8


def _nt_dot(a, b):
    return lax.dot_general(a, b, (((1,), (1,)), ((), ())), preferred_element_type=F32)


def _dot(a, b):
    return jnp.dot(a, b, preferred_element_type=F32)


def _sigmoid(x):
    return 1.0 / (1.0 + jnp.exp(-x))


def _layer_norm(x, g, b):
    mu = jnp.mean(x, axis=-1, keepdims=True)
    xc = x - mu
    var = jnp.mean(xc * xc, axis=-1, keepdims=True)
    return xc * lax.rsqrt(var + LN_EPS) * g + b


def _rope128(y, c, su, sd):
    return y * c + pltpu.roll(y, LANES - 32, 1) * su + pltpu.roll(y, 32, 1) * sd


def _compiler_params(semantics):
    return pltpu.CompilerParams(dimension_semantics=semantics, vmem_limit_bytes=VMEM_LIMIT_BYTES)


def _proj_nsa_kernel(x_ref, w_ref, cq_ref, suq_ref, sdq_ref, ck_ref, suk_ref, sdk_ref,
                     q_ref, kc_ref, vc_ref, ks_ref, vs_ref, kw_ref, vw_ref, g_ref):
    xb = x_ref[...].astype(BF16)
    a_q = q_ref.shape[1]
    _project_columns(xb, w_ref, 0, q_ref, (cq_ref, suq_ref, sdq_ref))
    outs = (kc_ref, vc_ref, ks_ref, vs_ref, kw_ref, vw_ref)
    for i, o_ref in enumerate(outs):
        tabs = (ck_ref, suk_ref, sdk_ref) if i % 2 == 0 else None
        _project_columns(xb, w_ref, a_q + i * A_KV, o_ref, tabs)
    c0 = a_q + 6 * A_KV
    g_ref[...] = _sigmoid(_dot(xb, w_ref[:, c0:c0 + LANES]))


def _proj_diff_kernel(x_ref, w_ref, cq_ref, suq_ref, sdq_ref, ck_ref, suk_ref, sdk_ref,
                      q_ref, k_ref, v_ref):
    xb = x_ref[...].astype(BF16)
    b_q = q_ref.shape[1]
    _project_columns(xb, w_ref, 0, q_ref, (cq_ref, suq_ref, sdq_ref))
    _project_columns(xb, w_ref, b_q, k_ref, (ck_ref, suk_ref, sdk_ref))
    _project_columns(xb, w_ref, b_q + B_KV, v_ref, None)


def _project_columns(xb, w_ref, col0, o_ref, rope_tabs):
    step = 2 * LANES
    for j in range(o_ref.shape[1] // step):
        y = _dot(xb, w_ref[:, col0 + j * step:col0 + (j + 1) * step])
        for h in range(2):
            yh = y[:, h * LANES:(h + 1) * LANES]
            if rope_tabs is not None:
                yh = _rope128(yh, *(t[...] for t in rope_tabs))
            o_ref[:, j * step + h * LANES:j * step + (h + 1) * LANES] = yh


def _rope_tables(pos, scale):
    half = HEAD_DIM_A // 2
    inv = ROPE_THETA ** (-jnp.arange(half, dtype=F32) / half)
    ang = pos.astype(F32)[:, None] * inv[None, :]
    cos, sin = jnp.cos(ang), jnp.sin(ang)
    zero = jnp.zeros_like(sin)
    c = jnp.concatenate([cos, cos, cos, cos], axis=1) * scale
    su = jnp.concatenate([-sin, zero, -sin, zero], axis=1) * scale
    sd = jnp.concatenate([zero, sin, zero, sin], axis=1) * scale
    return c, su, sd


def _token_tile(n_tok):
    return TOKEN_TILE if n_tok % TOKEN_TILE == 0 else n_tok


def _project(kernel_fn, x, w, tabs_q, tabs_k, out_widths):
    n_tok, d_model = x.shape
    tile = _token_tile(n_tok)
    assert tabs_q[0].shape[0] % tile == 0
    n_tab_blocks = tabs_q[0].shape[0] // tile
    tab_spec = pl.BlockSpec((tile, LANES), lambda i: (i % n_tab_blocks, 0))
    row_spec = lambda width: pl.BlockSpec((tile, width), lambda i: (i, 0))
    return pl.pallas_call(
        kernel_fn,
        grid=(n_tok // tile,),
        in_specs=[row_spec(d_model), pl.BlockSpec(w.shape, lambda i: (0, 0))] + [tab_spec] * 6,
        out_specs=[row_spec(wd) for wd in out_widths],
        out_shape=[jax.ShapeDtypeStruct((n_tok, wd), F32) for wd in out_widths],
        compiler_params=_compiler_params(("parallel",)),
        name=kernel_fn.__name__.strip("_"),
    )(x, w, *tabs_q, *tabs_k)


def _post_kernel(alpha, ff_chunk, o_ref, x_ref, p_ref, wo_ref, w1_ref, w2_ref, wg_ref, wp_ref,
                 g1_ref, b1_ref, g2_ref, b2_ref, out_ref):
    mix = _dot(o_ref[...].astype(BF16), wo_ref[...])
    h = _layer_norm(alpha * x_ref[...] + mix, g1_ref[...], b1_ref[...])
    hb = h.astype(BF16)
    d_ff = w1_ref.shape[1]
    f = None
    for c in range(d_ff // ff_chunk):
        a = _dot(hb, w1_ref[:, c * ff_chunk:(c + 1) * ff_chunk])
        a = jnp.square(jnp.maximum(a, 0.0)).astype(BF16)
        part = _dot(a, w2_ref[c * ff_chunk:(c + 1) * ff_chunk, :])
        f = part if f is None else f + part
    h2 = _layer_norm(alpha * h + f, g2_ref[...], b2_ref[...])
    gate = _sigmoid(_dot(h2.astype(BF16), wg_ref[...]))
    emb = _dot(p_ref[...].astype(BF16), wp_ref[...])
    out_ref[...] = h2 + gate * emb


def _post_block(o, x, p, wo, w1, w2, wg, wp, g1, b1, g2, b2, alpha):
    n_tok, d_model = x.shape
    tile = _token_tile(n_tok)
    row_spec = lambda width: pl.BlockSpec((tile, width), lambda i: (i, 0))
    const = lambda a: pl.BlockSpec(a.shape, lambda i: (0, 0), pipeline_mode=pl.Buffered(1))
    weights = (wo, w1, w2, wg, wp, g1, b1, g2, b2)
    return pl.pallas_call(
        functools.partial(_post_kernel, alpha, 1024),
        grid=(n_tok // tile,),
        in_specs=[row_spec(o.shape[1]), row_spec(d_model), row_spec(p.shape[1])]
        + [const(a) for a in weights],
        out_specs=row_spec(d_model),
        out_shape=jax.ShapeDtypeStruct((n_tok, d_model), F32),
        compiler_params=_compiler_params(("parallel",)),
        name="post_block",
    )(o, x, p, *weights)


def _gelu_tanh(x):
    return 0.5 * x * (1.0 + jnp.tanh(0.7978845608028654 * (x + 0.044715 * (x * x * x))))


def _compress_one(pages, w1cat_ref, w1flat_ref, pe_ref, b1_ref, w2bd_ref, out_ref):
    chunks_per_page = PAGE_SIZE // CMP_STRIDE
    n_chunks = len(pages) * chunks_per_page
    hidden = w2bd_ref.shape[0] // N_KV_A
    slabs = A_KV // LANES
    per_slab = N_KV_A // slabs
    parts = [None] * N_KV_A
    for s in range(CMP_STRIDE):
        for h in range(slabs):
            xs = jnp.concatenate(
                [pg[0, pl.ds(s * slabs + h, chunks_per_page, stride=CMP_STRIDE * slabs), :]
                 for pg in pages], axis=0)
            for gh in range(per_slab):
                g = h * per_slab + gh
                d = _dot(xs[:, gh * HEAD_DIM_A:(gh + 1) * HEAD_DIM_A].astype(BF16), w1cat_ref[s])
                parts[g] = d if parts[g] is None else parts[g] + d
    pe_rows = jnp.broadcast_to(pe_ref[...], (SUBLANES, pe_ref.shape[1])).astype(BF16)
    const = _dot(pe_rows, w1flat_ref[...])[0:1] + b1_ref[...]
    acts = []
    for g in range(N_KV_A):
        nxt = pltpu.roll(parts[g][:, hidden:], n_chunks - 1, 0)
        acts.append(_gelu_tanh(parts[g][:, :hidden] + nxt + const))
    act = jnp.concatenate(acts, axis=1).astype(BF16)
    out = _dot(act, w2bd_ref[...])
    row = lax.broadcasted_iota(jnp.int32, out.shape, 0)
    out_ref[0] = jnp.where(row < n_chunks - 1, out, 0.0)


def _compress_kernel(n_pages, pt_ref, *refs):
    del pt_ref
    k_pages = refs[:n_pages]
    v_pages = refs[n_pages:2 * n_pages]
    kw = refs[2 * n_pages:2 * n_pages + 5]
    vw = refs[2 * n_pages + 5:2 * n_pages + 10]
    kc_ref, vc_ref = refs[2 * n_pages + 10:]
    _compress_one(k_pages, *kw, kc_ref)
    _compress_one(v_pages, *vw, vc_ref)


def _compress_weights(w1, b1, pe, w2):
    r, stride, d, hidden = w1.shape
    w1cat = w1.transpose(1, 2, 0, 3).reshape(stride, d, r * hidden).astype(BF16)
    w1flat = w1.reshape(r * stride * d, hidden).astype(BF16)
    w2bd = jnp.kron(jnp.eye(N_KV_A, dtype=F32), w2).astype(BF16)
    return w1cat, w1flat, pe.reshape(1, r * stride * d), b1.reshape(1, hidden), w2bd


def _compress(k_pool, v_pool, page_table, k_weights, v_weights):
    n_req, n_pages = page_table.shape
    n_chunks = n_pages * PAGE_SIZE // CMP_STRIDE
    slab_rows = PAGE_SIZE * A_KV // LANES
    k_pool = k_pool.reshape(k_pool.shape[0], slab_rows, LANES)
    v_pool = v_pool.reshape(v_pool.shape[0], slab_rows, LANES)
    page_specs = [pl.BlockSpec((1, slab_rows, LANES), lambda b, pt, p=p: (pt[b, p], 0, 0))
                  for p in range(n_pages)]
    weights = tuple(k_weights) + tuple(v_weights)
    w_specs = [pl.BlockSpec(a.shape, lambda b, pt, nd=a.ndim: (0,) * nd) for a in weights]
    out_spec = pl.BlockSpec((1, n_chunks, A_KV), lambda b, pt: (b, 0, 0))
    grid_spec = pltpu.PrefetchScalarGridSpec(
        num_scalar_prefetch=1, grid=(n_req,),
        in_specs=page_specs + page_specs + w_specs, out_specs=[out_spec, out_spec])
    out_sds = jax.ShapeDtypeStruct((n_req, n_chunks, A_KV), F32)
    return pl.pallas_call(
        functools.partial(_compress_kernel, n_pages),
        grid_spec=grid_spec, out_shape=[out_sds, out_sds],
        compiler_params=_compiler_params(("parallel",)),
        name="nsa_compress",
    )(page_table, *([k_pool] * n_pages), *([v_pool] * n_pages), *weights)


def _overlap_matrix(n_cmp, n_sel):
    c_start = np.arange(n_cmp) * CMP_STRIDE
    j = np.arange(n_sel)
    ovl = ((c_start[:, None] < (j[None, :] + 1) * SEL_BLOCK)
           & (c_start[:, None] + CMP_BLOCK > j[None, :] * SEL_BLOCK)).astype(np.float32)
    out = np.zeros((LANES, LANES), np.float32)
    out[:n_cmp, :n_sel] = ovl
    return jnp.asarray(out, dtype=BF16)


def _compressed_probs(s, q_abs, n_cmp):
    n = lax.broadcasted_iota(jnp.int32, (1, LANES), 1)
    mask = ((n * CMP_STRIDE + (CMP_BLOCK - 1)) <= q_abs) & (n < n_cmp)
    s = jnp.where(mask[None], s, NEG_INF)
    m = jnp.max(s, axis=-1, keepdims=True)
    e = jnp.where(mask[None], jnp.exp(s - m), 0.0)
    l = jnp.sum(e, axis=-1, keepdims=True)
    return e * jnp.where(l > 0.0, 1.0 / l, 0.0)


def _select_blocks(imp, q_abs, n_sel):
    j = lax.broadcasted_iota(jnp.int32, (1, LANES), 1)
    cur = q_abs // SEL_BLOCK
    forced = (j == 0) | (j == cur) | (j == cur - 1)
    valid = (j * SEL_BLOCK) <= q_abs
    score = jnp.where(valid, imp + FORCE_SCORE * forced.astype(F32), -FORCE_SCORE * FORCE_SCORE)
    cnt = jnp.zeros(score.shape, F32)
    for i in range(n_sel):
        ci = score[:, i:i + 1]
        beats = (ci > score) | ((ci == score) & (i < j))
        cnt = cnt + jnp.where(beats, 1.0, 0.0)
    sel = (cnt < float(min(N_SELECT, n_sel))) & (j < n_sel)
    return jnp.where(sel, 1.0, 0.0).astype(BF16)


def _flash_loop(q, k_ref, v_ref, lo, hi, mask_fn, acc_ref, n_stack, rows):
    acc_ref[...] = jnp.zeros(acc_ref.shape, F32)

    def body(kc, carry):
        m, l = carry
        k0 = pl.multiple_of(kc * K_CHUNK, K_CHUNK)
        kb = k_ref[pl.ds(k0, K_CHUNK), :].astype(BF16)
        vb = v_ref[pl.ds(k0, K_CHUNK), :].astype(BF16)
        s = _nt_dot(q, kb).reshape(n_stack, rows, K_CHUNK)
        kpos = k0 + lax.broadcasted_iota(jnp.int32, (1, K_CHUNK), 1)
        mask = mask_fn(kpos)[None]
        s = jnp.where(mask, s, NEG_INF)
        m_new = jnp.maximum(m, jnp.max(s, axis=-1, keepdims=True))
        alpha = jnp.exp(m - m_new)
        p = jnp.where(mask, jnp.exp(s - m_new), 0.0)
        l_new = alpha * l + jnp.sum(p, axis=-1, keepdims=True)
        pv = _dot(p.reshape(n_stack * rows, K_CHUNK).astype(BF16), vb)
        acc_ref[...] = alpha.reshape(n_stack * rows, 1) * acc_ref[...] + pv
        return m_new, l_new

    m0 = jnp.full((n_stack, rows, 1), NEG_INF, F32)
    l0 = jnp.zeros((n_stack, rows, 1), F32)
    _, l = lax.fori_loop(lo, hi, body, (m0, l0))
    return l.reshape(n_stack * rows, 1)


def _nsa_prompt_kernel(n_cmp, n_sel, q_ref, gate_ref, kc_ref, vc_ref, ks_ref, vs_ref, kw_ref, vw_ref,
                       ovl_ref, o_ref, acc_ref):
    g = pl.program_id(1)
    qi = pl.program_id(2)
    tq = q_ref.shape[0]
    par = g % 2
    lane = lax.broadcasted_iota(jnp.int32, (1, LANES), 1)
    in_half = (lane // HEAD_DIM_A) == par
    q0 = qi * tq
    q_abs = q0 + lax.broadcasted_iota(jnp.int32, (tq, 1), 0)

    heads = []
    for r in range(REP_A):
        slab = q_ref[:, (r // 2) * LANES:(r // 2 + 1) * LANES]
        moved = jnp.where(par == (r % 2), slab, pltpu.roll(slab, HEAD_DIM_A, 1))
        heads.append(jnp.where(in_half, moved, 0.0))
    q = jnp.concatenate(heads, axis=0).astype(BF16)

    s_c = _nt_dot(q, kc_ref[0].astype(BF16)).reshape(REP_A, tq, LANES)
    p_c = _compressed_probs(s_c, q_abs, n_cmp)
    o_cmp = _dot(p_c.reshape(REP_A * tq, LANES).astype(BF16), vc_ref[0].astype(BF16))
    p_sum = p_c[0]
    for r in range(1, REP_A):
        p_sum = p_sum + p_c[r]
    imp = _dot(p_sum.astype(BF16), ovl_ref[...])
    sel01 = _select_blocks(imp, q_abs, n_sel)

    def gate_col(branch, r):
        col = branch * N_HEADS_A + g * REP_A + r
        return jnp.sum(jnp.where(lane == col, gate_ref[...], 0.0), axis=-1, keepdims=True)

    def gates(branch):
        return jnp.concatenate([gate_col(branch, r) for r in range(REP_A)], axis=0)

    total = gates(0) * o_cmp

    blk = lax.broadcasted_iota(jnp.int32, (LANES, 1), 0)

    def sel_mask(kpos):
        expand = jnp.where(blk == kpos // SEL_BLOCK, 1.0, 0.0).astype(BF16)
        picked = _dot(sel01, expand) > 0.5
        return picked & (kpos <= q_abs)

    hi = (q0 + tq + K_CHUNK - 1) // K_CHUNK
    l_s = _flash_loop(q, ks_ref, vs_ref, 0, hi, sel_mask, acc_ref, REP_A, tq)
    total = total + gates(1) * (acc_ref[...] / l_s)

    def win_mask(kpos):
        return (kpos <= q_abs) & (kpos >= q_abs - WINDOW)

    lo = jnp.maximum(q0 - WINDOW, 0) // K_CHUNK
    l_w = _flash_loop(q, kw_ref, vw_ref, lo, hi, win_mask, acc_ref, REP_A, tq)
    total = total + gates(2) * (acc_ref[...] / l_w)

    low = lane < HEAD_DIM_A
    for pr in range(REP_A // 2):
        even = total[(2 * pr) * tq:(2 * pr + 1) * tq]
        odd = total[(2 * pr + 1) * tq:(2 * pr + 2) * tq]
        even = jnp.where(par == 0, even, pltpu.roll(even, HEAD_DIM_A, 1))
        odd = jnp.where(par == 1, odd, pltpu.roll(odd, HEAD_DIM_A, 1))
        o_ref[:, pr * LANES:(pr + 1) * LANES] = jnp.where(low, even, odd)


def _nsa_prompt(q, gate, kc, vc, ks, vs, kw, vw, n_req, seq):
    n_tok = q.shape[0]
    nq = seq // Q_TILE
    n_cmp = seq // CMP_STRIDE - CMP_BLOCK // CMP_STRIDE + 1
    n_sel = -(-seq // SEL_BLOCK)
    group_w = REP_A * HEAD_DIM_A
    q_spec = pl.BlockSpec((Q_TILE, group_w), lambda b, g, i: (b * nq + i, g))
    gate_spec = pl.BlockSpec((Q_TILE, LANES), lambda b, g, i: (b * nq + i, 0))
    cmp_spec = pl.BlockSpec((1, kc.shape[1], LANES), lambda b, g, i: (b, 0, g // 2))
    kv_spec = pl.BlockSpec((seq, LANES), lambda b, g, i: (b, g // 2))
    ovl_spec = pl.BlockSpec((LANES, LANES), lambda b, g, i: (0, 0))
    return pl.pallas_call(
        functools.partial(_nsa_prompt_kernel, n_cmp, n_sel),
        grid=(n_req, N_KV_A, nq),
        in_specs=[q_spec, gate_spec, cmp_spec, cmp_spec, kv_spec, kv_spec, kv_spec, kv_spec, ovl_spec],
        out_specs=q_spec,
        out_shape=jax.ShapeDtypeStruct((n_tok, q.shape[1]), F32),
        scratch_shapes=[pltpu.VMEM((REP_A * Q_TILE, LANES), F32)],
        compiler_params=_compiler_params(("parallel", "parallel", "arbitrary")),
        name="nsa_prompt",
    )(q, gate, kc, vc, ks, vs, kw, vw, _overlap_matrix(n_cmp, n_sel))


def _lambda_value(lam_ref, lam_init):
    lv = lam_ref[...]
    s1 = jnp.sum(lv[0:1] * lv[1:2], axis=-1, keepdims=True)
    s2 = jnp.sum(lv[2:3] * lv[3:4], axis=-1, keepdims=True)
    return jnp.exp(s1) - jnp.exp(s2) + lam_init


def _sub_norm(o, gain, lam_init):
    return o * lax.rsqrt(jnp.mean(o * o, axis=-1, keepdims=True) + LN_EPS) * gain * (1.0 - lam_init)


def _diff_prompt_kernel(lam_init, q_ref, k_ref, v_ref, lam_ref, gain_ref, o_ref, acc_ref):
    qi = pl.program_id(2)
    tq = q_ref.shape[0]
    rep = q_ref.shape[1] // LANES
    lane = lax.broadcasted_iota(jnp.int32, (1, LANES), 1)
    q0 = qi * tq
    q_abs = q0 + lax.broadcasted_iota(jnp.int32, (tq, 1), 0)
    parts = []
    for r in range(rep):
        slab = q_ref[:, r * LANES:(r + 1) * LANES]
        for c in range(2):
            parts.append(jnp.where((lane // HEAD_DIM_B) == c, slab, 0.0))
    q = jnp.concatenate(parts, axis=0).astype(BF16)

    hi = (q0 + tq + K_CHUNK - 1) // K_CHUNK
    l = _flash_loop(q, k_ref, v_ref, 0, hi, lambda kpos: kpos <= q_abs, acc_ref, 2 * rep, tq)
    o = acc_ref[...] / l
    lam = _lambda_value(lam_ref, lam_init)
    for r in range(rep):
        a = o[(2 * r) * tq:(2 * r + 1) * tq] - lam * o[(2 * r + 1) * tq:(2 * r + 2) * tq]
        o_ref[:, r * LANES:(r + 1) * LANES] = _sub_norm(a, gain_ref[...], lam_init)


def _diff_prompt(q, k, v, lam_vecs, gain, lam_init, n_req, seq):
    n_tok, b_q = q.shape
    nq = seq // Q_TILE
    group_w = b_q // N_KV_B
    q_spec = pl.BlockSpec((Q_TILE, group_w), lambda b, g, i: (b * nq + i, g))
    kv_spec = pl.BlockSpec((seq, LANES), lambda b, g, i: (b, g))
    small = lambda a: pl.BlockSpec(a.shape, lambda b, g, i: (0, 0))
    return pl.pallas_call(
        functools.partial(_diff_prompt_kernel, lam_init),
        grid=(n_req, N_KV_B, nq),
        in_specs=[q_spec, kv_spec, kv_spec, small(lam_vecs), small(gain)],
        out_specs=q_spec,
        out_shape=jax.ShapeDtypeStruct((n_tok, b_q), F32),
        scratch_shapes=[pltpu.VMEM((2 * (group_w // LANES) * Q_TILE, LANES), F32)],
        compiler_params=_compiler_params(("parallel", "parallel", "arbitrary")),
        name="diff_prompt",
    )(q, k, v, lam_vecs, gain)


def _pad_rows(x, rows):
    return jnp.concatenate([x, jnp.zeros((rows - x.shape[0], x.shape[1]), x.dtype)], axis=0)


def _row_softmax(s, mask):
    s = jnp.where(mask, s, NEG_INF)
    m = jnp.max(s, axis=-1, keepdims=True)
    e = jnp.where(mask, jnp.exp(s - m), 0.0)
    return e / jnp.sum(e, axis=-1, keepdims=True)


def _paged_scores(q, pages, new_ref):
    cols = [_nt_dot(q, pg[0].astype(BF16)) for pg in pages]
    cols.append(_nt_dot(q, _pad_rows(new_ref[0], PAGE_SIZE).astype(BF16)))
    return jnp.concatenate(cols, axis=1)


def _paged_values(p, pages, new_ref):
    out = _dot(p[:, len(pages) * PAGE_SIZE:].astype(BF16), _pad_rows(new_ref[0], PAGE_SIZE).astype(BF16))
    for i, pg in enumerate(pages):
        out = out + _dot(p[:, i * PAGE_SIZE:(i + 1) * PAGE_SIZE].astype(BF16), pg[0].astype(BF16))
    return out


def _nsa_decode_kernel(n_pages, past_len, dec_seq, n_cmp, n_sel, pt_ref, *refs):
    del pt_ref
    q_ref, gate_ref, kc_ref, vc_ref = refs[:4]
    ks_pages = refs[4:4 + n_pages]
    vs_pages = refs[4 + n_pages:4 + 2 * n_pages]
    ks_new, vs_new, kwb_ref, vwb_ref, kw_new, vw_new, ovl_ref = refs[4 + 2 * n_pages:11 + 2 * n_pages]
    o_ref, kwo_ref, vwo_ref = refs[11 + 2 * n_pages:]

    rows = q_ref.shape[1]
    grp_rows = N_KV_A * DEC_ROWS
    q = q_ref[0].astype(BF16)
    t_of = lambda n: lax.broadcasted_iota(jnp.int32, (n, 1), 0) % DEC_ROWS
    q_abs = past_len + t_of(rows)
    q_abs_g = past_len + t_of(grp_rows)
    total_len = past_len + dec_seq

    p_c = _compressed_probs(_nt_dot(q, kc_ref[0].astype(BF16))[None], q_abs, n_cmp)[0]
    o_cmp = _dot(p_c.astype(BF16), vc_ref[0].astype(BF16))
    p_sum = p_c[0:grp_rows]
    for r in range(1, REP_A):
        p_sum = p_sum + p_c[r * grp_rows:(r + 1) * grp_rows]
    sel01 = _select_blocks(_dot(p_sum.astype(BF16), ovl_ref[...]), q_abs_g, n_sel)

    n_keys = (n_pages + 1) * PAGE_SIZE
    kpos = lax.broadcasted_iota(jnp.int32, (1, n_keys), 1)
    blk = lax.broadcasted_iota(jnp.int32, (LANES, 1), 0)
    expand = jnp.where(blk == kpos // SEL_BLOCK, 1.0, 0.0).astype(BF16)
    picked = jnp.concatenate([_dot(sel01, expand)] * REP_A, axis=0) > 0.5
    mask_s = picked & (kpos <= q_abs) & (kpos < total_len)
    p_s = _row_softmax(_paged_scores(q, ks_pages, ks_new), mask_s)
    o_sel = _paged_values(p_s, vs_pages, vs_new)

    win = kwb_ref.shape[1]
    wcol = lax.broadcasted_iota(jnp.int32, (1, win + PAGE_SIZE), 1)
    k_abs = jnp.where(wcol < win, past_len - win + wcol, past_len + (wcol - win))
    mask_w = (k_abs <= q_abs) & (k_abs >= q_abs - WINDOW) & (k_abs < total_len)
    s_w = jnp.concatenate([_nt_dot(q, kwb_ref[0].astype(BF16)),
                           _nt_dot(q, _pad_rows(kw_new[0], PAGE_SIZE).astype(BF16))], axis=1)
    p_w = _row_softmax(s_w, mask_w)
    o_win = (_dot(p_w[:, :win].astype(BF16), vwb_ref[0].astype(BF16))
             + _dot(p_w[:, win:].astype(BF16), _pad_rows(vw_new[0], PAGE_SIZE).astype(BF16)))

    gts = gate_ref[0]
    o_full = gts[:, 0:1] * o_cmp + gts[:, 1:2] * o_sel + gts[:, 2:3] * o_win

    lane_grp = lax.broadcasted_iota(jnp.int32, (1, A_KV), 1) // HEAD_DIM_A
    for r in range(REP_A):
        acc = None
        for g in range(N_KV_A):
            r0 = (r * N_KV_A + g) * DEC_ROWS
            piece = jnp.where(lane_grp == g, o_full[r0:r0 + DEC_ROWS], 0.0)
            acc = piece if acc is None else acc + piece
        o_ref[0, :, r * A_KV:(r + 1) * A_KV] = acc

    for buf_ref, new_ref, out_ref in ((kwb_ref, kw_new, kwo_ref), (vwb_ref, vw_new, vwo_ref)):
        shifted = pltpu.roll(buf_ref[0], win - dec_seq, 0)
        out_ref[0, 0:win - SUBLANES, :] = shifted[0:win - SUBLANES]
        tail_row = lax.broadcasted_iota(jnp.int32, (SUBLANES, 1), 0)
        new_tail = pltpu.roll(new_ref[0], SUBLANES - dec_seq, 0)
        out_ref[0, win - SUBLANES:win, :] = jnp.where(tail_row < SUBLANES - dec_seq,
                                                     shifted[win - SUBLANES:win], new_tail)


def _nsa_decode(q_bd, gate_cols, kc, vc, ks_pool, vs_pool, page_table, ks_new, vs_new,
                kw_buf, vw_buf, kw_new, vw_new, past_len, dec_seq):
    n_req, n_pages = page_table.shape
    rows = q_bd.shape[1]
    win = kw_buf.shape[1]
    total_len = past_len + dec_seq
    n_cmp = total_len // CMP_STRIDE - CMP_BLOCK // CMP_STRIDE + 1
    n_sel = -(-total_len // SEL_BLOCK)
    per_req = lambda a: pl.BlockSpec((1,) + a.shape[1:], lambda b, pt: (b, 0, 0))
    page_specs = [pl.BlockSpec((1, PAGE_SIZE, A_KV), lambda b, pt, p=p: (pt[b, p], 0, 0))
                  for p in range(n_pages)]
    ovl = _overlap_matrix(n_cmp, n_sel)
    ovl_spec = pl.BlockSpec(ovl.shape, lambda b, pt: (0, 0))
    o_sds = jax.ShapeDtypeStruct((n_req, DEC_ROWS, REP_A * A_KV), F32)
    win_sds = jax.ShapeDtypeStruct(kw_buf.shape, F32)
    grid_spec = pltpu.PrefetchScalarGridSpec(
        num_scalar_prefetch=1, grid=(n_req,),
        in_specs=[per_req(q_bd), per_req(gate_cols), per_req(kc), per_req(vc)] + page_specs + page_specs
        + [per_req(ks_new), per_req(vs_new), per_req(kw_buf), per_req(vw_buf),
           per_req(kw_new), per_req(vw_new), ovl_spec],
        out_specs=[per_req(o_sds), per_req(win_sds), per_req(win_sds)])
    return pl.pallas_call(
        functools.partial(_nsa_decode_kernel, n_pages, past_len, dec_seq, n_cmp, n_sel),
        grid_spec=grid_spec, out_shape=[o_sds, win_sds, win_sds],
        compiler_params=_compiler_params(("parallel",)),
        name="nsa_decode",
    )(page_table, q_bd, gate_cols, kc, vc, *([ks_pool] * n_pages), *([vs_pool] * n_pages),
      ks_new, vs_new, kw_buf, vw_buf, kw_new, vw_new, ovl)


def _diff_decode_kernel(n_pages, past_len, dec_seq, lam_init, pt_ref, *refs):
    del pt_ref
    q_ref = refs[0]
    k_pages = refs[1:1 + n_pages]
    v_pages = refs[1 + n_pages:1 + 2 * n_pages]
    k_new, v_new, lam_ref, gain_ref, o_ref = refs[1 + 2 * n_pages:]

    rows = q_ref.shape[1]
    half_rows = rows // 2
    q = q_ref[0].astype(BF16)
    q_abs = past_len + lax.broadcasted_iota(jnp.int32, (rows, 1), 0) % DEC_ROWS
    n_keys = (n_pages + 1) * PAGE_SIZE
    kpos = lax.broadcasted_iota(jnp.int32, (1, n_keys), 1)
    mask = (kpos <= q_abs) & (kpos < past_len + dec_seq)
    p = _row_softmax(_paged_scores(q, k_pages, k_new), mask)
    lam = _lambda_value(lam_ref, lam_init)
    a = p[0:half_rows] - lam * p[half_rows:rows]
    o_full = _paged_values(a, v_pages, v_new)
    e_dim = 2 * HEAD_DIM_B
    rep = half_rows // (N_KV_B * DEC_ROWS)
    for r in range(rep):
        for g in range(N_KV_B):
            r0 = (r * N_KV_B + g) * DEC_ROWS
            piece = o_full[r0:r0 + DEC_ROWS, g * e_dim:(g + 1) * e_dim]
            c0 = (r * N_KV_B + g) * e_dim
            o_ref[0, :, c0:c0 + e_dim] = _sub_norm(piece, gain_ref[...], lam_init)


def _diff_decode(q_bd, k_pool, v_pool, page_table, k_new, v_new, lam_vecs, gain, lam_init,
                 past_len, dec_seq):
    n_req, n_pages = page_table.shape
    per_req = lambda a: pl.BlockSpec((1,) + a.shape[1:], lambda b, pt: (b, 0, 0))
    page_specs = [pl.BlockSpec((1, PAGE_SIZE, B_KV), lambda b, pt, p=p: (pt[b, p], 0, 0))
                  for p in range(n_pages)]
    small = lambda a: pl.BlockSpec(a.shape, lambda b, pt: (0, 0))
    o_sds = jax.ShapeDtypeStruct((n_req, DEC_ROWS, q_bd.shape[1] // 2 // DEC_ROWS * 2 * HEAD_DIM_B), F32)
    grid_spec = pltpu.PrefetchScalarGridSpec(
        num_scalar_prefetch=1, grid=(n_req,),
        in_specs=[per_req(q_bd)] + page_specs + page_specs
        + [per_req(k_new), per_req(v_new), small(lam_vecs), small(gain)],
        out_specs=per_req(o_sds))
    return pl.pallas_call(
        functools.partial(_diff_decode_kernel, n_pages, past_len, dec_seq, lam_init),
        grid_spec=grid_spec, out_shape=o_sds,
        compiler_params=_compiler_params(("parallel",)),
        name="diff_decode",
    )(page_table, q_bd, *([k_pool] * n_pages), *([v_pool] * n_pages), k_new, v_new, lam_vecs, gain)


def _pad_tokens(x, axis):
    pad = [(0, 0)] * x.ndim
    pad[axis] = (0, DEC_ROWS - x.shape[axis])
    return jnp.pad(x, pad)


def _nsa_decode_queries(q, gate, n_req, dec_seq):
    q5 = _pad_tokens(q.reshape(n_req, dec_seq, N_KV_A, REP_A, HEAD_DIM_A), 1)
    q5 = q5.transpose(0, 3, 2, 1, 4)
    eye = jnp.eye(N_KV_A, dtype=F32)
    q_bd = q5[:, :, :, :, None, :] * eye[None, None, :, None, :, None]
    q_bd = q_bd.reshape(n_req, REP_A * N_KV_A * DEC_ROWS, A_KV)
    g5 = gate[:, :3 * N_HEADS_A].reshape(n_req, dec_seq, 3, N_KV_A, REP_A)
    g5 = _pad_tokens(g5, 1).transpose(0, 4, 3, 1, 2)
    g_cols = g5.reshape(n_req, REP_A * N_KV_A * DEC_ROWS, 3)
    g_cols = jnp.pad(g_cols, ((0, 0), (0, 0), (0, LANES - 3)))
    return q_bd, g_cols


def _diff_decode_queries(q, n_req, dec_seq, n_heads):
    rep = n_heads // N_KV_B
    q6 = _pad_tokens(q.reshape(n_req, dec_seq, N_KV_B, rep, 2, HEAD_DIM_B), 1)
    q6 = q6.transpose(0, 4, 3, 2, 1, 5)
    eye_g = jnp.eye(N_KV_B, dtype=F32)
    eye_c = jnp.eye(2, dtype=F32)
    q_bd = (q6[:, :, :, :, :, None, None, :]
            * eye_g[None, None, None, :, None, :, None, None]
            * eye_c[None, :, None, None, None, None, :, None])
    return q_bd.reshape(n_req, 2 * rep * N_KV_B * DEC_ROWS, B_KV)


def _new_rows(x, n_req, dec_seq):
    return _pad_tokens(x.reshape(n_req, dec_seq, x.shape[-1]), 1)


def _permute_out_rows(w_out, n_kv, rep):
    d = w_out.shape[0] // (n_kv * rep)
    return w_out.reshape(n_kv, rep, d, w_out.shape[1]).transpose(1, 0, 2, 3).reshape(w_out.shape)


def kernel(x_prompt, x_sample, cache_a_k_cmp, cache_a_v_cmp, cache_a_k_sel, cache_a_v_sel, state_a_k_win, state_a_v_win, cache_b_k, cache_b_v, page_table, p_prompt, p_sample, a_w_in, a_cmpk_w1, a_cmpk_b1, a_cmpk_pe, a_cmpk_w2, a_cmpv_w1, a_cmpv_b1, a_cmpv_pe, a_cmpv_w2, a_w_out, b_w_in, b_lambda, b_subln_g, b_w_out, ln1_g, ln1_b, ln2_g, ln2_b, mlp_w1, mlp_w2, ple_w, ple_gate_w):
    n_p, seq, d_model = x_prompt.shape
    n_s, dec_seq, _ = x_sample.shape
    depth = mlp_w1.shape[0]
    n_pool = cache_a_k_cmp.shape[1]
    n_pages = page_table.shape[1]
    past_len = n_pages * PAGE_SIZE
    win_buf = state_a_k_win.shape[2]
    a_q = N_HEADS_A * HEAD_DIM_A
    b_q = b_w_out.shape[1]
    n_heads_b = b_q // (2 * HEAD_DIM_B)
    alpha = (2 * depth) ** 0.25
    assert seq % TOKEN_TILE == 0 and (n_s * dec_seq) % SUBLANES == 0
    assert seq // CMP_STRIDE == LANES and past_len // CMP_STRIDE == LANES
    assert seq % PAGE_SIZE == 0 and seq % K_CHUNK == 0 and seq % Q_TILE == 0 and WINDOW % K_CHUNK == 0
    assert past_len % CMP_STRIDE == 0 and dec_seq < CMP_STRIDE and dec_seq <= DEC_ROWS
    assert win_buf == WINDOW and a_w_in.shape[2] == a_q + 6 * A_KV + 3 * N_HEADS_A

    xp = x_prompt.reshape(n_p * seq, d_model)
    xs = x_sample.reshape(n_s * dec_seq, d_model)
    pos_p = jnp.arange(seq)
    pos_s = past_len + (jnp.arange(n_s * dec_seq) % dec_seq)
    tabs_p = (_rope_tables(pos_p, HEAD_DIM_A ** -0.5), _rope_tables(pos_p, 1.0))
    tabs_s = (_rope_tables(pos_s, HEAD_DIM_A ** -0.5), _rope_tables(pos_s, 1.0))
    prompt_pages = jnp.arange(n_p * (seq // PAGE_SIZE), dtype=jnp.int32).reshape(n_p, seq // PAGE_SIZE)

    new_a_p, new_a_s, new_b_p, new_b_s = [], [], [], []
    for i in range(depth):
        j = i // 2
        if i % 2 == 0:
            w_in = jnp.pad(a_w_in[j], ((0, 0), (0, LANES - 3 * N_HEADS_A))).astype(BF16)
            widths = (a_q,) + (A_KV,) * 6 + (LANES,)
            pr = _project(_proj_nsa_kernel, xp, w_in, *tabs_p, widths)
            sr = _project(_proj_nsa_kernel, xs, w_in, *tabs_s, widths)
            kw_k = _compress_weights(a_cmpk_w1[j], a_cmpk_b1[j], a_cmpk_pe[j], a_cmpk_w2[j])
            kw_v = _compress_weights(a_cmpv_w1[j], a_cmpv_b1[j], a_cmpv_pe[j], a_cmpv_w2[j])
            as_pages = lambda t: t.reshape(n_p * seq // PAGE_SIZE, PAGE_SIZE, A_KV)
            kc_p, vc_p = _compress(as_pages(pr[1]), as_pages(pr[2]), prompt_pages, kw_k, kw_v)
            mix_p = _nsa_prompt(pr[0], pr[7], kc_p, vc_p, pr[3], pr[4], pr[5], pr[6], n_p, seq)
            wo_p = a_w_out[j].astype(BF16)
            pool = lambda c: c[j].reshape(n_pool, PAGE_SIZE, A_KV)
            kc_s, vc_s = _compress(pool(cache_a_k_cmp), pool(cache_a_v_cmp), page_table, kw_k, kw_v)
            q_bd, g_cols = _nsa_decode_queries(sr[0], sr[7], n_s, dec_seq)
            news = [_new_rows(t, n_s, dec_seq) for t in sr[1:7]]
            o_s, kwin, vwin = _nsa_decode(
                q_bd, g_cols, kc_s, vc_s, pool(cache_a_k_sel), pool(cache_a_v_sel), page_table,
                news[2], news[3], state_a_k_win[j].reshape(n_s, win_buf, A_KV),
                state_a_v_win[j].reshape(n_s, win_buf, A_KV), news[4], news[5], past_len, dec_seq)
            mix_s = o_s[:, :dec_seq].reshape(n_s * dec_seq, a_q)
            wo_s = _permute_out_rows(a_w_out[j], N_KV_A, REP_A).astype(BF16)
            head = lambda t, n, s: t.reshape(n, s, N_KV_A, HEAD_DIM_A)
            keep_p = min(WINDOW, seq)
            new_a_p.append(tuple(head(t, n_p, seq) for t in pr[1:5])
                           + tuple(head(t, n_p, seq)[:, seq - keep_p:] for t in pr[5:7]))
            new_a_s.append(tuple(head(t, n_s, dec_seq) for t in sr[1:5])
                           + (head(kwin, n_s, win_buf), head(vwin, n_s, win_buf)))
        else:
            lam_init = 0.8 - 0.6 * math.exp(-0.3 * i)
            w_in = b_w_in[j].astype(BF16)
            widths = (b_q, B_KV, B_KV)
            pr = _project(_proj_diff_kernel, xp, w_in, *tabs_p, widths)
            sr = _project(_proj_diff_kernel, xs, w_in, *tabs_s, widths)
            gain = b_subln_g[j].reshape(1, 2 * HEAD_DIM_B)
            mix_p = _diff_prompt(pr[0], pr[1], pr[2], b_lambda[j], gain, lam_init, n_p, seq)
            wo_p = b_w_out[j].astype(BF16)
            q_bd = _diff_decode_queries(sr[0], n_s, dec_seq, n_heads_b)
            pool = lambda c: c[j].reshape(n_pool, PAGE_SIZE, B_KV)
            o_s = _diff_decode(q_bd, pool(cache_b_k), pool(cache_b_v), page_table,
                               _new_rows(sr[1], n_s, dec_seq), _new_rows(sr[2], n_s, dec_seq),
                               b_lambda[j], gain, lam_init, past_len, dec_seq)
            mix_s = o_s[:, :dec_seq].reshape(n_s * dec_seq, b_q)
            wo_s = _permute_out_rows(b_w_out[j], N_KV_B, n_heads_b // N_KV_B).astype(BF16)
            head = lambda t, n, s: t.reshape(n, s, N_KV_B, 2 * HEAD_DIM_B)
            new_b_p.append((head(pr[1], n_p, seq), head(pr[2], n_p, seq)))
            new_b_s.append((head(sr[1], n_s, dec_seq), head(sr[2], n_s, dec_seq)))
        row = lambda v: v[i].reshape(1, d_model)
        shared = (mlp_w1[i].astype(BF16), mlp_w2[i].astype(BF16), ple_gate_w[i].astype(BF16),
                  ple_w[i].astype(BF16), row(ln1_g), row(ln1_b), row(ln2_g), row(ln2_b))
        xp = _post_block(mix_p, xp, p_prompt[i].reshape(n_p * seq, -1), wo_p, *shared, alpha)
        xs = _post_block(mix_s, xs, p_sample[i].reshape(n_s * dec_seq, -1), wo_s, *shared, alpha)

    stack = lambda parts: [jnp.stack(t) for t in zip(*parts)]
    return (xp.reshape(n_p, seq, d_model), xs.reshape(n_s, dec_seq, d_model),
            *stack(new_a_p), *stack(new_b_p), *stack(new_a_s), *stack(new_b_s))
```
